```python
import jax, jax.numpy as jnp
from jax import lax
import numpy as np

D_MODEL = 1024
BATCH = 4
SEQ = 8192
DEPTH = 1

CHUNK = 64
WIDTH_A = D_MODEL
A_GROUPS = 8
A_GROUP_DIM = WIDTH_A // A_GROUPS
SPATIAL_CHUNK = 128
WIDTH_B = D_MODEL
B_GROUPS = 8
CONV_WIDTH = 3
SEG_WIDTHS = [WIDTH_A, WIDTH_A, WIDTH_A,
              WIDTH_B, WIDTH_B, WIDTH_B, WIDTH_B,
              D_MODEL, D_MODEL]
PROJ_WIDTH = int(sum(SEG_WIDTHS))
SPLIT_POINTS = [int(v) for v in np.cumsum(SEG_WIDTHS)[:-1]]
EPS = 1e-6

kernel_name = "hybrid_gmlp_shortconv_gated_block"


def rmsnorm(x, g):
    xf = x.astype(jnp.float32)
    r = lax.rsqrt(jnp.mean(xf * xf, axis=-1, keepdims=True) + EPS)
    return (xf * r).astype(x.dtype) * g


def spatial_gating(u, v, v_norm_g, w_spatial, b_spatial):
    bsz, seq, _ = v.shape
    n_chunks = seq // SPATIAL_CHUNK
    u = jax.nn.gelu(u, approximate=False)
    v = rmsnorm(jax.nn.gelu(v, approximate=False), v_norm_g)
    v = v.reshape(bsz, n_chunks, SPATIAL_CHUNK, A_GROUPS, A_GROUP_DIM)
    tril = jnp.tril(jnp.ones((SPATIAL_CHUNK, SPATIAL_CHUNK), dtype=bool))
    w_s = jnp.where(tril[None], w_spatial, jnp.zeros((), w_spatial.dtype))
    mixed = jnp.einsum('gts,bnsgc->bntgc', w_s, v) + b_spatial.T[None, None, :, :, None]
    return u * mixed.reshape(bsz, seq, WIDTH_A)


def short_gated_conv(x_b, c_b, b_b, conv_w):
    seq = x_b.shape[1]
    hc = c_b * x_b
    padded = jnp.pad(hc, ((0, 0), (CONV_WIDTH - 1, 0), (0, 0)))
    conv = padded[:, 0:seq, :] * conv_w[0]
    for k in range(1, CONV_WIDTH):
        conv = conv + padded[:, k:k + seq, :] * conv_w[k]
    return b_b * conv


def setup_inputs(seed: int = 0) -> dict:
    key = jax.random.key(seed)
    ks = jax.random.split(key, 11)
    x = jax.random.normal(ks[0], (BATCH, SEQ, D_MODEL), jnp.float32)
    norm_g = 1.0 + 0.02 * jax.random.normal(ks[1], (DEPTH, D_MODEL), jnp.float32)
    w_in = jax.random.normal(ks[2], (DEPTH, D_MODEL, PROJ_WIDTH), jnp.float32) * D_MODEL ** -0.5
    v_norm_g = 1.0 + 0.02 * jax.random.normal(ks[3], (DEPTH, WIDTH_A), jnp.float32)
    w_spatial = jax.random.normal(ks[4], (DEPTH, A_GROUPS, SPATIAL_CHUNK, SPATIAL_CHUNK), jnp.float32) * (0.5 * SPATIAL_CHUNK ** -0.5)
    b_spatial = 1.0 + 0.02 * jax.random.normal(ks[5], (DEPTH, A_GROUPS, SPATIAL_CHUNK), jnp.float32)
    conv_w = jax.random.normal(ks[6], (DEPTH, CONV_WIDTH, WIDTH_B), jnp.float32) * CONV_WIDTH ** -0.5
    w_branch_a = jax.random.normal(ks[7], (DEPTH, WIDTH_A, D_MODEL), jnp.float32) * WIDTH_A ** -0.5
    w_branch_b = jax.random.normal(ks[8], (DEPTH, WIDTH_B, D_MODEL), jnp.float32) * WIDTH_B ** -0.5
    w_out = jax.random.normal(ks[9], (DEPTH, D_MODEL, D_MODEL), jnp.float32) * D_MODEL ** -0.5
    final_norm_g = 1.0 + 0.02 * jax.random.normal(ks[10], (D_MODEL,), jnp.float32)
    return {"x": x, "norm_g": norm_g, "w_in": w_in, "v_norm_g": v_norm_g,
            "w_spatial": w_spatial, "b_spatial": b_spatial, "conv_w": conv_w,
            "w_branch_a": w_branch_a, "w_branch_b": w_branch_b, "w_out": w_out,
            "final_norm_g": final_norm_g}


def reference(x, norm_g, w_in, v_norm_g, w_spatial, b_spatial, conv_w,
              w_branch_a, w_branch_b, w_out, final_norm_g):
    for layer in range(DEPTH):
        h = rmsnorm(x, norm_g[layer])
        proj = jnp.einsum('bsd,de->bse', h, w_in[layer])
        u_a, v_a, z_a, x_b, c_b, b_b, z_b, g_a, g_b = jnp.split(proj, SPLIT_POINTS, axis=-1)
        y_a = spatial_gating(u_a, v_a, v_norm_g[layer], w_spatial[layer], b_spatial[layer]) * jax.nn.silu(z_a)
        y_b = short_gated_conv(x_b, c_b, b_b, conv_w[layer]) * jax.nn.silu(z_b)
        ya_d = jnp.einsum('bse,ed->bsd', y_a, w_branch_a[layer])
        yb_d = jnp.einsum('bse,ed->bsd', y_b, w_branch_b[layer])
        merged = jax.nn.sigmoid(g_a) * ya_d + jax.nn.sigmoid(g_b) * yb_d
        x = x + jnp.einsum('bsd,de->bse', merged, w_out[layer])
    return rmsnorm(x, final_norm_g)
```

```python
import functools
import math

import jax
import jax.numpy as jnp
from jax import lax
from jax.experimental import pallas as pl
from jax.experimental.pallas import tpu as pltpu

D_MODEL = 1024
GROUPS = 8
GROUP_DIM = D_MODEL // GROUPS
SPATIAL_CHUNK = 128
CONV_WIDTH = 3
N_SEG = 9
EPS = 1e-6
SUBLANES = 8
TOKEN_TILE = 256
VMEM_LIMIT_BYTES = 56 * 1024 * 1024


def _rms_scale(x):
    return lax.rsqrt(jnp.mean(x * x, axis=-1, keepdims=True) + EPS)


def _gelu(x):
    return 0.5 * x * (1.0 + lax.erf(x * (1.0 / math.sqrt(2.0))))


def _sigmoid(x):
    return 1.0 / (1.0 + jnp.exp(-x))


def _silu(x):
    return x * _sigmoid(x)


def _block_kernel(x_ref, ng_ref, win_ref, vg_ref, ws_ref, bias_ref, cw_ref,
                  pa_ref, pb_ref, wo_ref, fg_ref, o_ref,
                  ws_scr, hc_scr, *, tiles_per_seq):
    i = pl.program_id(0)
    tm = x_ref.shape[0]

    @pl.when(i == 0)
    def _():
        row = lax.broadcasted_iota(jnp.int32, (SPATIAL_CHUNK, SPATIAL_CHUNK), 0)
        col = lax.broadcasted_iota(jnp.int32, (SPATIAL_CHUNK, SPATIAL_CHUNK), 1)
        for g in range(GROUPS):
            ws_scr[g] = jnp.where(row >= col, ws_ref[g], 0.0).astype(jnp.bfloat16)

    @pl.when(i % tiles_per_seq == 0)
    def _():
        hc_scr[0:SUBLANES, :] = jnp.zeros((SUBLANES, D_MODEL), jnp.float32)

    x = x_ref[...]
    h = (x * _rms_scale(x) * ng_ref[...]).astype(jnp.bfloat16)

    def proj(seg):
        return jnp.dot(h, win_ref[:, seg * D_MODEL:(seg + 1) * D_MODEL],
                       preferred_element_type=jnp.float32)

    v = _gelu(proj(1))
    vn = (v * _rms_scale(v) * vg_ref[...]).astype(jnp.bfloat16)
    chunks = []
    for n in range(tm // SPATIAL_CHUNK):
        rows = slice(n * SPATIAL_CHUNK, (n + 1) * SPATIAL_CHUNK)
        cols = [jnp.dot(ws_scr[g], vn[rows, g * GROUP_DIM:(g + 1) * GROUP_DIM],
                        preferred_element_type=jnp.float32) for g in range(GROUPS)]
        chunks.append(jnp.concatenate(cols, axis=1) + bias_ref[...])
    mixed = jnp.concatenate(chunks, axis=0)
    y_a = (_gelu(proj(0)) * mixed * _silu(proj(2))).astype(jnp.bfloat16)
    ya_d = jnp.dot(y_a, pa_ref[...], preferred_element_type=jnp.float32)

    hc_scr[SUBLANES:SUBLANES + tm, :] = proj(4) * proj(3)
    conv = hc_scr[SUBLANES - 2:SUBLANES - 2 + tm, :] * cw_ref[0:1, :]
    conv = conv + hc_scr[SUBLANES - 1:SUBLANES - 1 + tm, :] * cw_ref[1:2, :]
    conv = conv + hc_scr[SUBLANES:SUBLANES + tm, :] * cw_ref[2:3, :]
    hc_scr[0:SUBLANES, :] = hc_scr[tm:tm + SUBLANES, :]
    y_b = (proj(5) * conv * _silu(proj(6))).astype(jnp.bfloat16)
    yb_d = jnp.dot(y_b, pb_ref[...], preferred_element_type=jnp.float32)

    merged = (_sigmoid(proj(7)) * ya_d + _sigmoid(proj(8)) * yb_d).astype(jnp.bfloat16)
    y = x + jnp.dot(merged, wo_ref[...], preferred_element_type=jnp.float32)
    o_ref[...] = y * _rms_scale(y) * fg_ref[...]


def _resident(shape):
    return pl.BlockSpec(shape, lambda i: (0,) * len(shape), pipeline_mode=pl.Buffered(1))


def kernel(x, norm_g, w_in, v_norm_g, w_spatial, b_spatial, conv_w,
           w_branch_a, w_branch_b, w_out, final_norm_g):
    batch, seq, d = x.shape
    depth = norm_g.shape[0]
    assert d == D_MODEL and depth == 1 and seq % TOKEN_TILE == 0
    assert w_in.shape == (depth, D_MODEL, N_SEG * D_MODEL)
    tokens = batch * seq
    bf = jnp.bfloat16

    bias_full = jnp.repeat(b_spatial[0].T, GROUP_DIM, axis=1)
    row = lambda a: a.reshape(1, D_MODEL)

    out = pl.pallas_call(
        functools.partial(_block_kernel, tiles_per_seq=seq // TOKEN_TILE),
        grid=(tokens // TOKEN_TILE,),
        in_specs=[
            pl.BlockSpec((TOKEN_TILE, D_MODEL), lambda i: (i, 0)),
            _resident((1, D_MODEL)),
            _resident((D_MODEL, N_SEG * D_MODEL)),
            _resident((1, D_MODEL)),
            _resident((GROUPS, SPATIAL_CHUNK, SPATIAL_CHUNK)),
            _resident((SPATIAL_CHUNK, D_MODEL)),
            _resident((CONV_WIDTH, D_MODEL)),
            _resident((D_MODEL, D_MODEL)),
            _resident((D_MODEL, D_MODEL)),
            _resident((D_MODEL, D_MODEL)),
            _resident((1, D_MODEL)),
        ],
        out_specs=pl.BlockSpec((TOKEN_TILE, D_MODEL), lambda i: (i, 0)),
        out_shape=jax.ShapeDtypeStruct((tokens, D_MODEL), jnp.float32),
        scratch_shapes=[
            pltpu.VMEM((GROUPS, SPATIAL_CHUNK, SPATIAL_CHUNK), bf),
            pltpu.VMEM((TOKEN_TILE + SUBLANES, D_MODEL), jnp.float32),
        ],
        compiler_params=pltpu.CompilerParams(
            dimension_semantics=("arbitrary",),
            vmem_limit_bytes=VMEM_LIMIT_BYTES),
        name="gmlp_shortconv_block",
    )(x.reshape(tokens, D_MODEL), row(norm_g[0]), w_in[0].astype(bf), row(v_norm_g[0]),
      w_spatial[0], bias_full, conv_w[0], w_branch_a[0].astype(bf),
      w_branch_b[0].astype(bf), w_out[0].astype(bf), row(final_norm_g))
    return out.reshape(batch, seq, D_MODEL)
```

```python
import functools
import math

import jax
import jax.numpy as jnp
from jax import lax
from jax.experimental import pallas as pl
from jax.experimental.pallas import tpu as pltpu

D_MODEL = 1024
GROUPS = 8
GROUP_DIM = D_MODEL // GROUPS
SPATIAL_CHUNK = 128
CONV_WIDTH = 3
N_SEG = 9
EPS = 1e-6
SUBLANES = 8
TOKEN_TILE = 256
VMEM_LIMIT_BYTES = 56 * 1024 * 1024


def _rms_scale(x):
    return lax.rsqrt(jnp.mean(x * x, axis=-1, keepdims=True) + EPS)


def _gelu(x):
    return 0.5 * x * (1.0 + lax.erf(x * (1.0 / math.sqrt(2.0))))


def _sigmoid(x):
    return 1.0 / (1.0 + jnp.exp(-x))


def _silu(x):
    return x * _sigmoid(x)


def _block_kernel(x_ref, ng_ref, win_ref, vg_ref, ws_ref, bias_ref, cw_ref,
                  pa_ref, pb_ref, wo_ref, fg_ref, o_ref,
                  ws_scr, hc_scr, *, tiles_per_seq):
    i = pl.program_id(0)
    tm = x_ref.shape[0]

    @pl.when(i == 0)
    def _():
        row = lax.broadcasted_iota(jnp.int32, (SPATIAL_CHUNK, SPATIAL_CHUNK), 0)
        col = lax.broadcasted_iota(jnp.int32, (SPATIAL_CHUNK, SPATIAL_CHUNK), 1)
        for g in range(GROUPS):
            ws_scr[g] = jnp.where(row >= col, ws_ref[g], 0.0).astype(jnp.bfloat16)

    @pl.when(i % tiles_per_seq == 0)
    def _():
        hc_scr[0:SUBLANES, :] = jnp.zeros((SUBLANES, D_MODEL), jnp.float32)

    x = x_ref[...]
    h = (x * _rms_scale(x) * ng_ref[...]).astype(jnp.bfloat16)

    def proj(seg):
        return jnp.dot(h, win_ref[:, seg * D_MODEL:(seg + 1) * D_MODEL],
                       preferred_element_type=jnp.float32)


    v = _gelu(proj(1))
    ug = _gelu(proj(0)) * _silu(proj(2))
    vn = (v * _rms_scale(v) * vg_ref[...]).astype(jnp.bfloat16)
    chunks = []
    for n in range(tm // SPATIAL_CHUNK):
        rows = slice(n * SPATIAL_CHUNK, (n + 1) * SPATIAL_CHUNK)
        cols = [jnp.dot(ws_scr[g], vn[rows, g * GROUP_DIM:(g + 1) * GROUP_DIM],
                        preferred_element_type=jnp.float32) for g in range(GROUPS)]
        chunks.append(jnp.concatenate(cols, axis=1) + bias_ref[...])
    mixed = jnp.concatenate(chunks, axis=0)

    hc_scr[SUBLANES:SUBLANES + tm, :] = proj(4) * proj(3)
    y_a = (ug * mixed).astype(jnp.bfloat16)
    bg = proj(5) * _silu(proj(6))
    ya_d = jnp.dot(y_a, pa_ref[...], preferred_element_type=jnp.float32)
    conv = hc_scr[SUBLANES - 2:SUBLANES - 2 + tm, :] * cw_ref[0:1, :]
    conv = conv + hc_scr[SUBLANES - 1:SUBLANES - 1 + tm, :] * cw_ref[1:2, :]
    conv = conv + hc_scr[SUBLANES:SUBLANES + tm, :] * cw_ref[2:3, :]
    hc_scr[0:SUBLANES, :] = hc_scr[tm:tm + SUBLANES, :]
    y_b = (bg * conv).astype(jnp.bfloat16)
    sig_a = _sigmoid(proj(7))
    sig_b = _sigmoid(proj(8))
    yb_d = jnp.dot(y_b, pb_ref[...], preferred_element_type=jnp.float32)

    merged = (sig_a * ya_d + sig_b * yb_d).astype(jnp.bfloat16)
    y = x + jnp.dot(merged, wo_ref[...], preferred_element_type=jnp.float32)
    o_ref[...] = y * _rms_scale(y) * fg_ref[...]


def _resident(shape):
    return pl.BlockSpec(shape, lambda i: (0,) * len(shape), pipeline_mode=pl.Buffered(1))


def kernel(x, norm_g, w_in, v_norm_g, w_spatial, b_spatial, conv_w,
           w_branch_a, w_branch_b, w_out, final_norm_g):
    batch, seq, d = x.shape
    depth = norm_g.shape[0]
    assert d == D_MODEL and depth == 1 and seq % TOKEN_TILE == 0
    assert w_in.shape == (depth, D_MODEL, N_SEG * D_MODEL)
    tokens = batch * seq
    bf = jnp.bfloat16

    bias_full = jnp.repeat(b_spatial[0].T, GROUP_DIM, axis=1)
    row = lambda a: a.reshape(1, D_MODEL)

    out = pl.pallas_call(
        functools.partial(_block_kernel, tiles_per_seq=seq // TOKEN_TILE),
        grid=(tokens // TOKEN_TILE,),
        in_specs=[
            pl.BlockSpec((TOKEN_TILE, D_MODEL), lambda i: (i, 0)),
            _resident((1, D_MODEL)),
            _resident((D_MODEL, N_SEG * D_MODEL)),
            _resident((1, D_MODEL)),
            _resident((GROUPS, SPATIAL_CHUNK, SPATIAL_CHUNK)),
            _resident((SPATIAL_CHUNK, D_MODEL)),
            _resident((CONV_WIDTH, D_MODEL)),
            _resident((D_MODEL, D_MODEL)),
            _resident((D_MODEL, D_MODEL)),
            _resident((D_MODEL, D_MODEL)),
            _resident((1, D_MODEL)),
        ],
        out_specs=pl.BlockSpec((TOKEN_TILE, D_MODEL), lambda i: (i, 0)),
        out_shape=jax.ShapeDtypeStruct((tokens, D_MODEL), jnp.float32),
        scratch_shapes=[
            pltpu.VMEM((GROUPS, SPATIAL_CHUNK, SPATIAL_CHUNK), bf),
            pltpu.VMEM((TOKEN_TILE + SUBLANES, D_MODEL), jnp.float32),
        ],
        compiler_params=pltpu.CompilerParams(
            dimension_semantics=("arbitrary",),
            vmem_limit_bytes=VMEM_LIMIT_BYTES),
        name="gmlp_shortconv_block",
    )(x.reshape(tokens, D_MODEL), row(norm_g[0]), w_in[0].astype(bf), row(v_norm_g[0]),
      w_spatial[0], bias_full, conv_w[0], w_branch_a[0].astype(bf),
      w_branch_b[0].astype(bf), w_out[0].astype(bf), row(final_norm_g))
    return out.reshape(batch, seq, D_MODEL)
```

```python
import functools
import math

import jax
import jax.numpy as jnp
from jax import lax
from jax.experimental import pallas as pl
from jax.experimental.pallas import tpu as pltpu

D_MODEL = 1024
GROUPS = 8
GROUP_DIM = D_MODEL // GROUPS
SPATIAL_CHUNK = 128
CONV_WIDTH = 3
N_SEG = 9
EPS = 1e-6
SUBLANES = 8
SUB_ROWS = 256
TOKEN_TILE = 2 * SUB_ROWS
VMEM_LIMIT_BYTES = 56 * 1024 * 1024


def _rms_scale(x):
    return lax.rsqrt(jnp.mean(x * x, axis=-1, keepdims=True) + EPS)


def _gelu(x):
    return 0.5 * x * (1.0 + lax.erf(x * (1.0 / math.sqrt(2.0))))


def _sigmoid(x):
    return 1.0 / (1.0 + jnp.exp(-x))


def _silu(x):
    return x * _sigmoid(x)


def _block_kernel(x_ref, ng_ref, win_ref, vg_ref, ws_ref, bias_ref, cw_ref,
                  pa_ref, pb_ref, wo_ref, fg_ref, o_ref,
                  ws_scr, hc_scr, *, tiles_per_seq):
    i = pl.program_id(0)
    tm = x_ref.shape[0]

    @pl.when(i == 0)
    def _():
        row = lax.broadcasted_iota(jnp.int32, (SPATIAL_CHUNK, SPATIAL_CHUNK), 0)
        col = lax.broadcasted_iota(jnp.int32, (SPATIAL_CHUNK, SPATIAL_CHUNK), 1)
        for g in range(GROUPS):
            ws_scr[g] = jnp.where(row >= col, ws_ref[g], 0.0).astype(jnp.bfloat16)

    @pl.when(i % tiles_per_seq == 0)
    def _():
        hc_scr[0:SUBLANES, :] = jnp.zeros((SUBLANES, D_MODEL), jnp.float32)

    starts = range(0, tm, SUB_ROWS)

    def dot(a, b):
        return jnp.dot(a, b, preferred_element_type=jnp.float32)

    def proj(h, seg):
        return dot(h, win_ref[:, seg * D_MODEL:(seg + 1) * D_MODEL])

    def spatial_mix(vn):
        chunks = []
        for n in range(SUB_ROWS // SPATIAL_CHUNK):
            rows = slice(n * SPATIAL_CHUNK, (n + 1) * SPATIAL_CHUNK)
            cols = [dot(ws_scr[g], vn[rows, g * GROUP_DIM:(g + 1) * GROUP_DIM])
                    for g in range(GROUPS)]
            chunks.append(jnp.concatenate(cols, axis=1) + bias_ref[...])
        return jnp.concatenate(chunks, axis=0)

    def conv3(r0):
        taps = [hc_scr[SUBLANES - 2 + k + r0:SUBLANES - 2 + k + r0 + SUB_ROWS, :] * cw_ref[k:k + 1, :]
                for k in range(CONV_WIDTH)]
        return taps[0] + taps[1] + taps[2]

    xs = [x_ref[r0:r0 + SUB_ROWS, :] for r0 in starts]
    hs = [(x * _rms_scale(x) * ng_ref[...]).astype(jnp.bfloat16) for x in xs]

    vs = [_gelu(proj(h, 1)) for h in hs]
    ugs = [_gelu(proj(h, 0)) * _silu(proj(h, 2)) for h in hs]
    vns = [(v * _rms_scale(v) * vg_ref[...]).astype(jnp.bfloat16) for v in vs]
    mixeds = [spatial_mix(vn) for vn in vns]

    for r0, h in zip(starts, hs):
        hc_scr[SUBLANES + r0:SUBLANES + r0 + SUB_ROWS, :] = proj(h, 4) * proj(h, 3)
    y_as = [(ug * mixed).astype(jnp.bfloat16) for ug, mixed in zip(ugs, mixeds)]
    bgs = [proj(h, 5) * _silu(proj(h, 6)) for h in hs]
    ya_ds = [dot(y_a, pa_ref[...]) for y_a in y_as]
    y_bs = [(bg * conv3(r0)).astype(jnp.bfloat16) for r0, bg in zip(starts, bgs)]
    hc_scr[0:SUBLANES, :] = hc_scr[tm:tm + SUBLANES, :]
    sigs = [(_sigmoid(proj(h, 7)), _sigmoid(proj(h, 8))) for h in hs]
    yb_ds = [dot(y_b, pb_ref[...]) for y_b in y_bs]

    mergeds = [(sa * ya_d + sb * yb_d).astype(jnp.bfloat16)
               for (sa, sb), ya_d, yb_d in zip(sigs, ya_ds, yb_ds)]
    for r0, x, merged in zip(starts, xs, mergeds):
        y = x + dot(merged, wo_ref[...])
        o_ref[r0:r0 + SUB_ROWS, :] = y * _rms_scale(y) * fg_ref[...]


def _resident(shape):
    return pl.BlockSpec(shape, lambda i: (0,) * len(shape), pipeline_mode=pl.Buffered(1))


def kernel(x, norm_g, w_in, v_norm_g, w_spatial, b_spatial, conv_w,
           w_branch_a, w_branch_b, w_out, final_norm_g):
    batch, seq, d = x.shape
    depth = norm_g.shape[0]
    assert d == D_MODEL and depth == 1 and seq % TOKEN_TILE == 0
    assert w_in.shape == (depth, D_MODEL, N_SEG * D_MODEL)
    tokens = batch * seq
    bf = jnp.bfloat16

    bias_full = jnp.repeat(b_spatial[0].T, GROUP_DIM, axis=1)
    row = lambda a: a.reshape(1, D_MODEL)

    out = pl.pallas_call(
        functools.partial(_block_kernel, tiles_per_seq=seq // TOKEN_TILE),
        grid=(tokens // TOKEN_TILE,),
        in_specs=[
            pl.BlockSpec((TOKEN_TILE, D_MODEL), lambda i: (i, 0)),
            _resident((1, D_MODEL)),
            _resident((D_MODEL, N_SEG * D_MODEL)),
            _resident((1, D_MODEL)),
            _resident((GROUPS, SPATIAL_CHUNK, SPATIAL_CHUNK)),
            _resident((SPATIAL_CHUNK, D_MODEL)),
            _resident((CONV_WIDTH, D_MODEL)),
            _resident((D_MODEL, D_MODEL)),
            _resident((D_MODEL, D_MODEL)),
            _resident((D_MODEL, D_MODEL)),
            _resident((1, D_MODEL)),
        ],
        out_specs=pl.BlockSpec((TOKEN_TILE, D_MODEL), lambda i: (i, 0)),
        out_shape=jax.ShapeDtypeStruct((tokens, D_MODEL), jnp.float32),
        scratch_shapes=[
            pltpu.VMEM((GROUPS, SPATIAL_CHUNK, SPATIAL_CHUNK), bf),
            pltpu.VMEM((TOKEN_TILE + SUBLANES, D_MODEL), jnp.float32),
        ],
        compiler_params=pltpu.CompilerParams(
            dimension_semantics=("arbitrary",),
            vmem_limit_bytes=VMEM_LIMIT_BYTES),
        name="gmlp_shortconv_block",
    )(x.reshape(tokens, D_MODEL), row(norm_g[0]), w_in[0].astype(bf), row(v_norm_g[0]),
      w_spatial[0], bias_full, conv_w[0], w_branch_a[0].astype(bf),
      w_branch_b[0].astype(bf), w_out[0].astype(bf), row(final_norm_g))
    return out.reshape(batch, seq, D_MODEL)
```

```python
import functools
import math

import jax
import jax.numpy as jnp
from jax import lax
from jax.experimental import pallas as pl
from jax.experimental.pallas import tpu as pltpu

D_MODEL = 1024
GROUPS = 8
GROUP_DIM = D_MODEL // GROUPS
SPATIAL_CHUNK = 128
CONV_WIDTH = 3
N_SEG = 9
EPS = 1e-6
SUBLANES = 8
SUB_ROWS = 256
TOKEN_TILE = 2 * SUB_ROWS
VMEM_LIMIT_BYTES = 56 * 1024 * 1024


def _rms_scale(x):
    return lax.rsqrt(jnp.mean(x * x, axis=-1, keepdims=True) + EPS)


def _gelu(x):
    return 0.5 * x * (1.0 + lax.erf(x * (1.0 / math.sqrt(2.0))))


def _sigmoid(x):
    return 1.0 / (1.0 + jnp.exp(-x))


def _silu(x):
    return x * _sigmoid(x)


def _block_kernel(x_ref, ng_ref, win_ref, vg_ref, ws_ref, bias_ref, cw_ref,
                  pa_ref, pb_ref, wo_ref, fg_ref, o_ref,
                  ws_scr, hc_scr, *, tiles_per_seq):
    i = pl.program_id(0)
    tm = x_ref.shape[0]

    @pl.when(i == 0)
    def _():
        row = lax.broadcasted_iota(jnp.int32, (SPATIAL_CHUNK, SPATIAL_CHUNK), 0)
        col = lax.broadcasted_iota(jnp.int32, (SPATIAL_CHUNK, SPATIAL_CHUNK), 1)
        for g in range(GROUPS):
            ws_scr[g] = jnp.where(row >= col, ws_ref[g], 0.0).astype(jnp.bfloat16)

    @pl.when(i % tiles_per_seq == 0)
    def _():
        hc_scr[0:SUBLANES, :] = jnp.zeros((SUBLANES, D_MODEL), jnp.float32)

    starts = range(0, tm, SUB_ROWS)

    def dot(a, b):
        return jnp.dot(a, b, preferred_element_type=jnp.float32)

    def weight(w_ref, cols=slice(None)):
        return pltpu.bitcast(w_ref[:, cols], jnp.bfloat16)

    def proj(h, seg):
        return dot(h, weight(win_ref, slice(seg * D_MODEL, (seg + 1) * D_MODEL)))

    def spatial_mix(vn):
        chunks = []
        for n in range(SUB_ROWS // SPATIAL_CHUNK):
            rows = slice(n * SPATIAL_CHUNK, (n + 1) * SPATIAL_CHUNK)
            cols = [dot(ws_scr[g], vn[rows, g * GROUP_DIM:(g + 1) * GROUP_DIM])
                    for g in range(GROUPS)]
            chunks.append(jnp.concatenate(cols, axis=1) + bias_ref[...])
        return jnp.concatenate(chunks, axis=0)

    def conv3(r0):
        taps = [hc_scr[SUBLANES - 2 + k + r0:SUBLANES - 2 + k + r0 + SUB_ROWS, :] * cw_ref[k:k + 1, :]
                for k in range(CONV_WIDTH)]
        return taps[0] + taps[1] + taps[2]

    xs = [x_ref[r0:r0 + SUB_ROWS, :] for r0 in starts]
    hs = [(x * _rms_scale(x) * ng_ref[...]).astype(jnp.bfloat16) for x in xs]

    vs = [_gelu(proj(h, 1)) for h in hs]
    ugs = [_gelu(proj(h, 0)) * _silu(proj(h, 2)) for h in hs]
    vns = [(v * _rms_scale(v) * vg_ref[...]).astype(jnp.bfloat16) for v in vs]
    mixeds = [spatial_mix(vn) for vn in vns]

    for r0, h in zip(starts, hs):
        hc_scr[SUBLANES + r0:SUBLANES + r0 + SUB_ROWS, :] = proj(h, 4) * proj(h, 3)
    y_as = [(ug * mixed).astype(jnp.bfloat16) for ug, mixed in zip(ugs, mixeds)]
    bgs = [proj(h, 5) * _silu(proj(h, 6)) for h in hs]
    ya_ds = [dot(y_a, weight(pa_ref)) for y_a in y_as]
    y_bs = [(bg * conv3(r0)).astype(jnp.bfloat16) for r0, bg in zip(starts, bgs)]
    hc_scr[0:SUBLANES, :] = hc_scr[tm:tm + SUBLANES, :]
    sigs = [(_sigmoid(proj(h, 7)), _sigmoid(proj(h, 8))) for h in hs]
    yb_ds = [dot(y_b, weight(pb_ref)) for y_b in y_bs]

    mergeds = [(sa * ya_d + sb * yb_d).astype(jnp.bfloat16)
               for (sa, sb), ya_d, yb_d in zip(sigs, ya_ds, yb_ds)]
    for r0, x, merged in zip(starts, xs, mergeds):
        y = x + dot(merged, weight(wo_ref))
        o_ref[r0:r0 + SUB_ROWS, :] = y * _rms_scale(y) * fg_ref[...]


def _resident(shape):
    return pl.BlockSpec(shape, lambda i: (0,) * len(shape), pipeline_mode=pl.Buffered(1))


def _pack_row_pairs(w):
    halves = lax.bitcast_convert_type(w.astype(jnp.bfloat16), jnp.uint16).astype(jnp.uint32)
    return halves[0::2] | (halves[1::2] << 16)


def kernel(x, norm_g, w_in, v_norm_g, w_spatial, b_spatial, conv_w,
           w_branch_a, w_branch_b, w_out, final_norm_g):
    batch, seq, d = x.shape
    depth = norm_g.shape[0]
    assert d == D_MODEL and depth == 1 and seq % TOKEN_TILE == 0
    assert w_in.shape == (depth, D_MODEL, N_SEG * D_MODEL)
    tokens = batch * seq
    bf = jnp.bfloat16

    bias_full = jnp.repeat(b_spatial[0].T, GROUP_DIM, axis=1)
    row = lambda a: a.reshape(1, D_MODEL)

    out = pl.pallas_call(
        functools.partial(_block_kernel, tiles_per_seq=seq // TOKEN_TILE),
        grid=(tokens // TOKEN_TILE,),
        in_specs=[
            pl.BlockSpec((TOKEN_TILE, D_MODEL), lambda i: (i, 0)),
            _resident((1, D_MODEL)),
            _resident((D_MODEL // 2, N_SEG * D_MODEL)),
            _resident((1, D_MODEL)),
            _resident((GROUPS, SPATIAL_CHUNK, SPATIAL_CHUNK)),
            _resident((SPATIAL_CHUNK, D_MODEL)),
            _resident((CONV_WIDTH, D_MODEL)),
            _resident((D_MODEL // 2, D_MODEL)),
            _resident((D_MODEL // 2, D_MODEL)),
            _resident((D_MODEL // 2, D_MODEL)),
            _resident((1, D_MODEL)),
        ],
        out_specs=pl.BlockSpec((TOKEN_TILE, D_MODEL), lambda i: (i, 0)),
        out_shape=jax.ShapeDtypeStruct((tokens, D_MODEL), jnp.float32),
        scratch_shapes=[
            pltpu.VMEM((GROUPS, SPATIAL_CHUNK, SPATIAL_CHUNK), bf),
            pltpu.VMEM((TOKEN_TILE + SUBLANES, D_MODEL), jnp.float32),
        ],
        compiler_params=pltpu.CompilerParams(
            dimension_semantics=("arbitrary",),
            vmem_limit_bytes=VMEM_LIMIT_BYTES),
        name="gmlp_shortconv_block",
    )(x.reshape(tokens, D_MODEL), row(norm_g[0]), _pack_row_pairs(w_in[0]), row(v_norm_g[0]),
      w_spatial[0], bias_full, conv_w[0], _pack_row_pairs(w_branch_a[0]),
      _pack_row_pairs(w_branch_b[0]), _pack_row_pairs(w_out[0]), row(final_norm_g))
    return out.reshape(batch, seq, D_MODEL)
```

```python
import functools
import math

import jax
import jax.numpy as jnp
from jax import lax
from jax.experimental import pallas as pl
from jax.experimental.pallas import tpu as pltpu

D_MODEL = 1024
GROUPS = 8
GROUP_DIM = D_MODEL // GROUPS
SPATIAL_CHUNK = 128
CONV_WIDTH = 3
N_SEG = 9
EPS = 1e-6
SUBLANES = 8
SUB_ROWS = 256
TOKEN_TILE = 2 * SUB_ROWS
WIN_STAGE_ROWS = 64
SQ_STAGE_ROWS = 256
VMEM_LIMIT_BYTES = 56 * 1024 * 1024


def _rms_scale(x):
    return lax.rsqrt(jnp.mean(x * x, axis=-1, keepdims=True) + EPS)


def _gelu(x):
    return 0.5 * x * (1.0 + lax.erf(x * (1.0 / math.sqrt(2.0))))


def _sigmoid(x):
    return 1.0 / (1.0 + jnp.exp(-x))


def _silu(x):
    return x * _sigmoid(x)


def _stage_weights(jobs, sem):
    copies = [pltpu.make_async_copy(src, stage.at[k % 2], sem.at[k % 2])
              for k, (src, stage, _, _) in enumerate(jobs)]
    copies[0].start()
    for k, (_, stage, dst, rows) in enumerate(jobs):
        if k + 1 < len(jobs):
            copies[k + 1].start()
        copies[k].wait()
        dst[rows, :] = stage[k % 2].astype(jnp.bfloat16)


def _block_kernel(x_ref, ng_ref, vg_ref, ws_ref, bias_ref, cw_ref, fg_ref,
                  win_hbm, pa_hbm, pb_hbm, wo_hbm, o_ref,
                  win_scr, pa_scr, pb_scr, wo_scr, ws_scr, hc_scr,
                  win_stage, sq_stage, stage_sem, *, tiles_per_seq):
    i = pl.program_id(0)
    tm = x_ref.shape[0]

    @pl.when(i == 0)
    def _():
        jobs = [(win_hbm.at[pl.ds(r, WIN_STAGE_ROWS)], win_stage, win_scr, pl.ds(r, WIN_STAGE_ROWS))
                for r in range(0, D_MODEL, WIN_STAGE_ROWS)]
        for src, dst in ((pa_hbm, pa_scr), (pb_hbm, pb_scr), (wo_hbm, wo_scr)):
            jobs += [(src.at[pl.ds(r, SQ_STAGE_ROWS)], sq_stage, dst, pl.ds(r, SQ_STAGE_ROWS))
                     for r in range(0, D_MODEL, SQ_STAGE_ROWS)]
        _stage_weights(jobs, stage_sem)
        row = lax.broadcasted_iota(jnp.int32, (SPATIAL_CHUNK, SPATIAL_CHUNK), 0)
        col = lax.broadcasted_iota(jnp.int32, (SPATIAL_CHUNK, SPATIAL_CHUNK), 1)
        for g in range(GROUPS):
            ws_scr[g] = jnp.where(row >= col, ws_ref[g], 0.0).astype(jnp.bfloat16)

    @pl.when(i % tiles_per_seq == 0)
    def _():
        hc_scr[0:SUBLANES, :] = jnp.zeros((SUBLANES, D_MODEL), jnp.float32)

    starts = range(0, tm, SUB_ROWS)

    def dot(a, b):
        return jnp.dot(a, b, preferred_element_type=jnp.float32)

    def proj(h, seg):
        return dot(h, win_scr[:, seg * D_MODEL:(seg + 1) * D_MODEL])

    def spatial_mix(vn):
        chunks = []
        for n in range(SUB_ROWS // SPATIAL_CHUNK):
            rows = slice(n * SPATIAL_CHUNK, (n + 1) * SPATIAL_CHUNK)
            cols = [dot(ws_scr[g], vn[rows, g * GROUP_DIM:(g + 1) * GROUP_DIM])
                    for g in range(GROUPS)]
            chunks.append(jnp.concatenate(cols, axis=1) + bias_ref[...])
        return jnp.concatenate(chunks, axis=0)

    def conv3(r0):
        taps = [hc_scr[SUBLANES - 2 + k + r0:SUBLANES - 2 + k + r0 + SUB_ROWS, :] * cw_ref[k:k + 1, :]
                for k in range(CONV_WIDTH)]
        return taps[0] + taps[1] + taps[2]

    xs = [x_ref[r0:r0 + SUB_ROWS, :] for r0 in starts]
    hs = [(x * _rms_scale(x) * ng_ref[...]).astype(jnp.bfloat16) for x in xs]

    vs = [_gelu(proj(h, 1)) for h in hs]
    ugs = [_gelu(proj(h, 0)) * _silu(proj(h, 2)) for h in hs]
    vns = [(v * _rms_scale(v) * vg_ref[...]).astype(jnp.bfloat16) for v in vs]
    mixeds = [spatial_mix(vn) for vn in vns]

    for r0, h in zip(starts, hs):
        hc_scr[SUBLANES + r0:SUBLANES + r0 + SUB_ROWS, :] = proj(h, 4) * proj(h, 3)
    y_as = [(ug * mixed).astype(jnp.bfloat16) for ug, mixed in zip(ugs, mixeds)]
    bgs = [proj(h, 5) * _silu(proj(h, 6)) for h in hs]
    ya_ds = [dot(y_a, pa_scr[...]) for y_a in y_as]
    y_bs = [(bg * conv3(r0)).astype(jnp.bfloat16) for r0, bg in zip(starts, bgs)]
    hc_scr[0:SUBLANES, :] = hc_scr[tm:tm + SUBLANES, :]
    sigs = [(_sigmoid(proj(h, 7)), _sigmoid(proj(h, 8))) for h in hs]
    yb_ds = [dot(y_b, pb_scr[...]) for y_b in y_bs]

    mergeds = [(sa * ya_d + sb * yb_d).astype(jnp.bfloat16)
               for (sa, sb), ya_d, yb_d in zip(sigs, ya_ds, yb_ds)]
    for r0, x, merged in zip(starts, xs, mergeds):
        y = x + dot(merged, wo_scr[...])
        o_ref[r0:r0 + SUB_ROWS, :] = y * _rms_scale(y) * fg_ref[...]


def _resident(shape):
    return pl.BlockSpec(shape, lambda i: (0,) * len(shape), pipeline_mode=pl.Buffered(1))


def kernel(x, norm_g, w_in, v_norm_g, w_spatial, b_spatial, conv_w,
           w_branch_a, w_branch_b, w_out, final_norm_g):
    batch, seq, d = x.shape
    depth = norm_g.shape[0]
    assert d == D_MODEL and depth == 1 and seq % TOKEN_TILE == 0
    assert w_in.shape == (depth, D_MODEL, N_SEG * D_MODEL)
    tokens = batch * seq
    bf = jnp.bfloat16

    bias_full = jnp.repeat(b_spatial[0].T, GROUP_DIM, axis=1)
    row = lambda a: a.reshape(1, D_MODEL)
    in_hbm = pl.BlockSpec(memory_space=pl.ANY)

    out = pl.pallas_call(
        functools.partial(_block_kernel, tiles_per_seq=seq // TOKEN_TILE),
        grid=(tokens // TOKEN_TILE,),
        in_specs=[
            pl.BlockSpec((TOKEN_TILE, D_MODEL), lambda i: (i, 0)),
            _resident((1, D_MODEL)),
            _resident((1, D_MODEL)),
            _resident((GROUPS, SPATIAL_CHUNK, SPATIAL_CHUNK)),
            _resident((SPATIAL_CHUNK, D_MODEL)),
            _resident((CONV_WIDTH, D_MODEL)),
            _resident((1, D_MODEL)),
            in_hbm, in_hbm, in_hbm, in_hbm,
        ],
        out_specs=pl.BlockSpec((TOKEN_TILE, D_MODEL), lambda i: (i, 0)),
        out_shape=jax.ShapeDtypeStruct((tokens, D_MODEL), jnp.float32),
        scratch_shapes=[
            pltpu.VMEM((D_MODEL, N_SEG * D_MODEL), bf),
            pltpu.VMEM((D_MODEL, D_MODEL), bf),
            pltpu.VMEM((D_MODEL, D_MODEL), bf),
            pltpu.VMEM((D_MODEL, D_MODEL), bf),
            pltpu.VMEM((GROUPS, SPATIAL_CHUNK, SPATIAL_CHUNK), bf),
            pltpu.VMEM((TOKEN_TILE + SUBLANES, D_MODEL), jnp.float32),
            pltpu.VMEM((2, WIN_STAGE_ROWS, N_SEG * D_MODEL), jnp.float32),
            pltpu.VMEM((2, SQ_STAGE_ROWS, D_MODEL), jnp.float32),
            pltpu.SemaphoreType.DMA((2,)),
        ],
        compiler_params=pltpu.CompilerParams(
            dimension_semantics=("arbitrary",),
            vmem_limit_bytes=VMEM_LIMIT_BYTES),
        name="gmlp_shortconv_block",
    )(x.reshape(tokens, D_MODEL), row(norm_g[0]), row(v_norm_g[0]), w_spatial[0], bias_full,
      conv_w[0], row(final_norm_g), w_in[0], w_branch_a[0], w_branch_b[0], w_out[0])
    return out.reshape(batch, seq, D_MODEL)
```

```python
import functools
import math

import jax
import jax.numpy as jnp
from jax import lax
from jax.experimental import pallas as pl
from jax.experimental.pallas import tpu as pltpu

D_MODEL = 1024
GROUPS = 8
GROUP_DIM = D_MODEL // GROUPS
SPATIAL_CHUNK = 128
CONV_WIDTH = 3
N_SEG = 9
EPS = 1e-6
SUBLANES = 8
SUB_ROWS = 256
TOKEN_TILE = 2 * SUB_ROWS
STAGE_SLOTS = 4
WIN_STAGE_ROWS = 32
SQ_STAGE_ROWS = 256
VMEM_LIMIT_BYTES = 56 * 1024 * 1024


def _rms_scale(x):
    return lax.rsqrt(jnp.mean(x * x, axis=-1, keepdims=True) + EPS)


def _gelu(x):
    return 0.5 * x * (1.0 + lax.erf(x * (1.0 / math.sqrt(2.0))))


def _sigmoid(x):
    return 1.0 / (1.0 + jnp.exp(-x))


def _silu(x):
    return x * _sigmoid(x)


def _stage_weights(jobs, sem):
    copies = [pltpu.make_async_copy(src, stage.at[k % STAGE_SLOTS], sem.at[k % STAGE_SLOTS])
              for k, (src, stage, _, _) in enumerate(jobs)]
    ahead = STAGE_SLOTS - 1
    for copy in copies[:ahead]:
        copy.start()
    for k, (_, stage, dst, rows) in enumerate(jobs):
        if k + ahead < len(jobs):
            copies[k + ahead].start()
        copies[k].wait()
        dst[rows, :] = stage[k % STAGE_SLOTS].astype(jnp.bfloat16)


def _block_kernel(x_ref, ng_ref, vg_ref, ws_ref, bias_ref, cw_ref, fg_ref,
                  win_hbm, pa_hbm, pb_hbm, wo_hbm, o_ref,
                  win_scr, pa_scr, pb_scr, wo_scr, ws_scr, hc_scr,
                  win_stage, sq_stage, stage_sem, *, tiles_per_seq):
    i = pl.program_id(0)
    tm = x_ref.shape[0]

    @pl.when(i == 0)
    def _():
        jobs = [(win_hbm.at[pl.ds(r, WIN_STAGE_ROWS)], win_stage, win_scr, pl.ds(r, WIN_STAGE_ROWS))
                for r in range(0, D_MODEL, WIN_STAGE_ROWS)]
        for src, dst in ((pa_hbm, pa_scr), (pb_hbm, pb_scr), (wo_hbm, wo_scr)):
            jobs += [(src.at[pl.ds(r, SQ_STAGE_ROWS)], sq_stage, dst, pl.ds(r, SQ_STAGE_ROWS))
                     for r in range(0, D_MODEL, SQ_STAGE_ROWS)]
        _stage_weights(jobs, stage_sem)
        row = lax.broadcasted_iota(jnp.int32, (SPATIAL_CHUNK, SPATIAL_CHUNK), 0)
        col = lax.broadcasted_iota(jnp.int32, (SPATIAL_CHUNK, SPATIAL_CHUNK), 1)
        for g in range(GROUPS):
            ws_scr[g] = jnp.where(row >= col, ws_ref[g], 0.0).astype(jnp.bfloat16)

    @pl.when(i % tiles_per_seq == 0)
    def _():
        hc_scr[0:SUBLANES, :] = jnp.zeros((SUBLANES, D_MODEL), jnp.float32)

    starts = range(0, tm, SUB_ROWS)

    def dot(a, b):
        return jnp.dot(a, b, preferred_element_type=jnp.float32)

    def proj(h, seg):
        return dot(h, win_scr[:, seg * D_MODEL:(seg + 1) * D_MODEL])

    def spatial_mix(vn):
        chunks = []
        for n in range(SUB_ROWS // SPATIAL_CHUNK):
            rows = slice(n * SPATIAL_CHUNK, (n + 1) * SPATIAL_CHUNK)
            cols = [dot(ws_scr[g], vn[rows, g * GROUP_DIM:(g + 1) * GROUP_DIM])
                    for g in range(GROUPS)]
            chunks.append(jnp.concatenate(cols, axis=1) + bias_ref[...])
        return jnp.concatenate(chunks, axis=0)

    def conv3(r0):
        taps = [hc_scr[SUBLANES - 2 + k + r0:SUBLANES - 2 + k + r0 + SUB_ROWS, :] * cw_ref[k:k + 1, :]
                for k in range(CONV_WIDTH)]
        return taps[0] + taps[1] + taps[2]

    xs = [x_ref[r0:r0 + SUB_ROWS, :] for r0 in starts]
    hs = [(x * _rms_scale(x) * ng_ref[...]).astype(jnp.bfloat16) for x in xs]

    vs = [_gelu(proj(h, 1)) for h in hs]
    ugs = [_gelu(proj(h, 0)) * _silu(proj(h, 2)) for h in hs]
    vns = [(v * _rms_scale(v) * vg_ref[...]).astype(jnp.bfloat16) for v in vs]
    mixeds = [spatial_mix(vn) for vn in vns]

    for r0, h in zip(starts, hs):
        hc_scr[SUBLANES + r0:SUBLANES + r0 + SUB_ROWS, :] = proj(h, 4) * proj(h, 3)
    y_as = [(ug * mixed).astype(jnp.bfloat16) for ug, mixed in zip(ugs, mixeds)]
    bgs = [proj(h, 5) * _silu(proj(h, 6)) for h in hs]
    ya_ds = [dot(y_a, pa_scr[...]) for y_a in y_as]
    y_bs = [(bg * conv3(r0)).astype(jnp.bfloat16) for r0, bg in zip(starts, bgs)]
    hc_scr[0:SUBLANES, :] = hc_scr[tm:tm + SUBLANES, :]
    sigs = [(_sigmoid(proj(h, 7)), _sigmoid(proj(h, 8))) for h in hs]
    yb_ds = [dot(y_b, pb_scr[...]) for y_b in y_bs]

    mergeds = [(sa * ya_d + sb * yb_d).astype(jnp.bfloat16)
               for (sa, sb), ya_d, yb_d in zip(sigs, ya_ds, yb_ds)]
    for r0, x, merged in zip(starts, xs, mergeds):
        y = x + dot(merged, wo_scr[...])
        o_ref[r0:r0 + SUB_ROWS, :] = y * _rms_scale(y) * fg_ref[...]


def _resident(shape):
    return pl.BlockSpec(shape, lambda i: (0,) * len(shape), pipeline_mode=pl.Buffered(1))


def kernel(x, norm_g, w_in, v_norm_g, w_spatial, b_spatial, conv_w,
           w_branch_a, w_branch_b, w_out, final_norm_g):
    batch, seq, d = x.shape
    depth = norm_g.shape[0]
    assert d == D_MODEL and depth == 1 and seq % TOKEN_TILE == 0
    assert w_in.shape == (depth, D_MODEL, N_SEG * D_MODEL)
    tokens = batch * seq
    bf = jnp.bfloat16

    bias_full = jnp.repeat(b_spatial[0].T, GROUP_DIM, axis=1)
    row = lambda a: a.reshape(1, D_MODEL)
    in_hbm = pl.BlockSpec(memory_space=pl.ANY)

    out = pl.pallas_call(
        functools.partial(_block_kernel, tiles_per_seq=seq // TOKEN_TILE),
        grid=(tokens // TOKEN_TILE,),
        in_specs=[
            pl.BlockSpec((TOKEN_TILE, D_MODEL), lambda i: (i, 0)),
            _resident((1, D_MODEL)),
            _resident((1, D_MODEL)),
            _resident((GROUPS, SPATIAL_CHUNK, SPATIAL_CHUNK)),
            _resident((SPATIAL_CHUNK, D_MODEL)),
            _resident((CONV_WIDTH, D_MODEL)),
            _resident((1, D_MODEL)),
            in_hbm, in_hbm, in_hbm, in_hbm,
        ],
        out_specs=pl.BlockSpec((TOKEN_TILE, D_MODEL), lambda i: (i, 0)),
        out_shape=jax.ShapeDtypeStruct((tokens, D_MODEL), jnp.float32),
        scratch_shapes=[
            pltpu.VMEM((D_MODEL, N_SEG * D_MODEL), bf),
            pltpu.VMEM((D_MODEL, D_MODEL), bf),
            pltpu.VMEM((D_MODEL, D_MODEL), bf),
            pltpu.VMEM((D_MODEL, D_MODEL), bf),
            pltpu.VMEM((GROUPS, SPATIAL_CHUNK, SPATIAL_CHUNK), bf),
            pltpu.VMEM((TOKEN_TILE + SUBLANES, D_MODEL), jnp.float32),
            pltpu.VMEM((STAGE_SLOTS, WIN_STAGE_ROWS, N_SEG * D_MODEL), jnp.float32),
            pltpu.VMEM((STAGE_SLOTS, SQ_STAGE_ROWS, D_MODEL), jnp.float32),
            pltpu.SemaphoreType.DMA((STAGE_SLOTS,)),
        ],
        compiler_params=pltpu.CompilerParams(
            dimension_semantics=("arbitrary",),
            vmem_limit_bytes=VMEM_LIMIT_BYTES),
        name="gmlp_shortconv_block",
    )(x.reshape(tokens, D_MODEL), row(norm_g[0]), row(v_norm_g[0]), w_spatial[0], bias_full,
      conv_w[0], row(final_norm_g), w_in[0], w_branch_a[0], w_branch_b[0], w_out[0])
    return out.reshape(batch, seq, D_MODEL)
```

```python
import functools
import math

import jax
import jax.numpy as jnp
from jax import lax
from jax.experimental import pallas as pl
from jax.experimental.pallas import tpu as pltpu

D_MODEL = 1024
GROUPS = 8
GROUP_DIM = D_MODEL // GROUPS
SPATIAL_CHUNK = 128
CONV_WIDTH = 3
N_SEG = 9
EPS = 1e-6
SUBLANES = 8
SUB_ROWS = 256
TOKEN_TILE = 2 * SUB_ROWS
STAGE_ROWS = 512
STAGE_SLOTS = 3
VMEM_LIMIT_BYTES = 58 * 1024 * 1024

SEG_U, SEG_V, SEG_ZA, SEG_XB, SEG_CB, SEG_BB, SEG_ZB, SEG_GA, SEG_GB = range(N_SEG)
W_PA, W_PB, W_OUT = N_SEG, N_SEG + 1, N_SEG + 2
USE_ORDER = (SEG_V, SEG_U, SEG_ZA, SEG_CB, SEG_XB, SEG_BB, SEG_ZB, W_PA, SEG_GA, SEG_GB, W_PB, W_OUT)


def _rms_scale(x):
    return lax.rsqrt(jnp.mean(x * x, axis=-1, keepdims=True) + EPS)


def _gelu(x):
    return 0.5 * x * (1.0 + lax.erf(x * (1.0 / math.sqrt(2.0))))


def _sigmoid(x):
    return 1.0 / (1.0 + jnp.exp(-x))


def _silu(x):
    return x * _sigmoid(x)


def _dot(a, b):
    return jnp.dot(a, b, preferred_element_type=jnp.float32)


class _WeightStream:
    def __init__(self, hbm_blocks, bf16_blocks, stage, sem):
        self._jobs = [(w, pl.ds(r, STAGE_ROWS)) for w in USE_ORDER
                      for r in range(0, D_MODEL, STAGE_ROWS)]
        self._dst = bf16_blocks
        self._stage = stage
        self._copies = [pltpu.make_async_copy(hbm_blocks[w].at[rows], stage.at[k % STAGE_SLOTS],
                                              sem.at[k % STAGE_SLOTS])
                        for k, (w, rows) in enumerate(self._jobs)]
        self._next = 0
        for copy in self._copies[:STAGE_SLOTS]:
            copy.start()

    def fetch(self, w):
        for _ in range(D_MODEL // STAGE_ROWS):
            k = self._next
            job_w, rows = self._jobs[k]
            assert job_w == w, "weights must be fetched in USE_ORDER"
            self._next += 1
            self._copies[k].wait()
            self._dst[w][rows, :] = self._stage[k % STAGE_SLOTS].astype(jnp.bfloat16)
            if k + STAGE_SLOTS < len(self._jobs):
                self._copies[k + STAGE_SLOTS].start()

    def done(self):
        assert self._next == len(self._jobs)


def _tile_body(x_ref, xn_ref, ng_ref, vg_ref, bias_ref, cw_ref, fg_ref, o_ref, wts, ws_scr, hc_scr,
               ha_scr, va_scr, fetch, first):
    tm = x_ref.shape[0]
    starts = range(0, tm, SUB_ROWS)

    def spatial_mix(vn):
        chunks = []
        for n in range(SUB_ROWS // SPATIAL_CHUNK):
            rows = slice(n * SPATIAL_CHUNK, (n + 1) * SPATIAL_CHUNK)
            cols = [_dot(ws_scr[g], vn[rows, g * GROUP_DIM:(g + 1) * GROUP_DIM])
                    for g in range(GROUPS)]
            chunks.append(jnp.concatenate(cols, axis=1) + bias_ref[...])
        return jnp.concatenate(chunks, axis=0)

    def conv3(r0):
        taps = [hc_scr[SUBLANES - 2 + k + r0:SUBLANES - 2 + k + r0 + SUB_ROWS, :] * cw_ref[k:k + 1, :]
                for k in range(CONV_WIDTH)]
        return taps[0] + taps[1] + taps[2]

    def normed(x):
        return (x * _rms_scale(x) * ng_ref[...]).astype(jnp.bfloat16)

    xs = [x_ref[r0:r0 + SUB_ROWS, :] for r0 in starts]
    hs = [normed(x) if first or k > 0 else ha_scr[...] for k, x in enumerate(xs)]

    fetch(SEG_V)
    v_raws = [_dot(hs[0], wts[SEG_V][...]) if first else va_scr[...]]
    fetch(SEG_U)
    fetch(SEG_ZA)
    ugs = []
    for k, h in enumerate(hs):
        if k > 0:
            v_raws.append(_dot(h, wts[SEG_V][...]))
        ugs.append(_gelu(_dot(h, wts[SEG_U][...])) * _silu(_dot(h, wts[SEG_ZA][...])))
    vs = [_gelu(v) for v in v_raws]
    vns = [(v * _rms_scale(v) * vg_ref[...]).astype(jnp.bfloat16) for v in vs]

    fetch(SEG_CB)
    fetch(SEG_XB)
    mixeds = []
    for r0, h, vn in zip(starts, hs, vns):
        mixeds.append(spatial_mix(vn))
        hc_scr[SUBLANES + r0:SUBLANES + r0 + SUB_ROWS, :] = (
            _dot(h, wts[SEG_CB][...]) * _dot(h, wts[SEG_XB][...]))
    y_as = [(ug * mixed).astype(jnp.bfloat16) for ug, mixed in zip(ugs, mixeds)]
    fetch(SEG_BB)
    fetch(SEG_ZB)
    bgs = [_dot(h, wts[SEG_BB][...]) * _silu(_dot(h, wts[SEG_ZB][...])) for h in hs]
    fetch(W_PA)
    ya_ds = [_dot(y_a, wts[W_PA][...]) for y_a in y_as]
    y_bs = [(bg * conv3(r0)).astype(jnp.bfloat16) for r0, bg in zip(starts, bgs)]
    hc_scr[0:SUBLANES, :] = hc_scr[tm:tm + SUBLANES, :]
    fetch(SEG_GA)
    fetch(SEG_GB)
    sigs = [(_sigmoid(_dot(h, wts[SEG_GA][...])), _sigmoid(_dot(h, wts[SEG_GB][...]))) for h in hs]
    fetch(W_PB)
    yb_ds = [_dot(y_b, wts[W_PB][...]) for y_b in y_bs]

    mergeds = [(sa * ya_d + sb * yb_d).astype(jnp.bfloat16)
               for (sa, sb), ya_d, yb_d in zip(sigs, ya_ds, yb_ds)]
    fetch(W_OUT)
    for r0, x, merged in zip(starts, xs, mergeds):
        y = x + _dot(merged, wts[W_OUT][...])
        o_ref[r0:r0 + SUB_ROWS, :] = y * _rms_scale(y) * fg_ref[...]

    h_next = normed(xn_ref[...])
    ha_scr[...] = h_next
    va_scr[...] = _dot(h_next, wts[SEG_V][...])


def _block_kernel(x_ref, xn_ref, ng_ref, vg_ref, ws_ref, bias_ref, cw_ref, fg_ref,
                  win_hbm, pa_hbm, pb_hbm, wo_hbm, o_ref,
                  win_scr, pa_scr, pb_scr, wo_scr, ws_scr, hc_scr, ha_scr, va_scr,
                  stage, stage_sem, *, tiles_per_seq):
    i = pl.program_id(0)
    seg_cols = lambda ref, s: ref.at[:, pl.ds(s * D_MODEL, D_MODEL)]
    wts = [seg_cols(win_scr, s) for s in range(N_SEG)] + [pa_scr, pb_scr, wo_scr]
    body = functools.partial(_tile_body, x_ref, xn_ref, ng_ref, vg_ref, bias_ref, cw_ref, fg_ref,
                             o_ref, wts, ws_scr, hc_scr, ha_scr, va_scr)

    @pl.when(i % tiles_per_seq == 0)
    def _():
        hc_scr[0:SUBLANES, :] = jnp.zeros((SUBLANES, D_MODEL), jnp.float32)

    @pl.when(i == 0)
    def _():
        row = lax.broadcasted_iota(jnp.int32, (SPATIAL_CHUNK, SPATIAL_CHUNK), 0)
        col = lax.broadcasted_iota(jnp.int32, (SPATIAL_CHUNK, SPATIAL_CHUNK), 1)
        for g in range(GROUPS):
            ws_scr[g] = jnp.where(row >= col, ws_ref[g], 0.0).astype(jnp.bfloat16)
        hbm = [seg_cols(win_hbm, s) for s in range(N_SEG)] + [pa_hbm, pb_hbm, wo_hbm]
        stream = _WeightStream(hbm, wts, stage, stage_sem)
        body(stream.fetch, first=True)
        stream.done()

    @pl.when(i != 0)
    def _():
        body(lambda w: None, first=False)


def _resident(shape):
    return pl.BlockSpec(shape, lambda i: (0,) * len(shape), pipeline_mode=pl.Buffered(1))


def kernel(x, norm_g, w_in, v_norm_g, w_spatial, b_spatial, conv_w,
           w_branch_a, w_branch_b, w_out, final_norm_g):
    batch, seq, d = x.shape
    depth = norm_g.shape[0]
    assert d == D_MODEL and depth == 1 and seq % TOKEN_TILE == 0
    assert w_in.shape == (depth, D_MODEL, N_SEG * D_MODEL)
    tokens = batch * seq
    bf = jnp.bfloat16

    bias_full = jnp.repeat(b_spatial[0].T, GROUP_DIM, axis=1)
    row = lambda a: a.reshape(1, D_MODEL)
    in_hbm = pl.BlockSpec(memory_space=pl.ANY)
    n_tiles = tokens // TOKEN_TILE
    blocks_per_tile = TOKEN_TILE // SUB_ROWS
    x2d = x.reshape(tokens, D_MODEL)

    out = pl.pallas_call(
        functools.partial(_block_kernel, tiles_per_seq=seq // TOKEN_TILE),
        grid=(n_tiles,),
        in_specs=[
            pl.BlockSpec((TOKEN_TILE, D_MODEL), lambda i: (i, 0)),
            pl.BlockSpec((SUB_ROWS, D_MODEL),
                         lambda i: (jnp.minimum(i + 1, n_tiles - 1) * blocks_per_tile, 0)),
            _resident((1, D_MODEL)),
            _resident((1, D_MODEL)),
            _resident((GROUPS, SPATIAL_CHUNK, SPATIAL_CHUNK)),
            _resident((SPATIAL_CHUNK, D_MODEL)),
            _resident((CONV_WIDTH, D_MODEL)),
            _resident((1, D_MODEL)),
            in_hbm, in_hbm, in_hbm, in_hbm,
        ],
        out_specs=pl.BlockSpec((TOKEN_TILE, D_MODEL), lambda i: (i, 0)),
        out_shape=jax.ShapeDtypeStruct((tokens, D_MODEL), jnp.float32),
        scratch_shapes=[
            pltpu.VMEM((D_MODEL, N_SEG * D_MODEL), bf),
            pltpu.VMEM((D_MODEL, D_MODEL), bf),
            pltpu.VMEM((D_MODEL, D_MODEL), bf),
            pltpu.VMEM((D_MODEL, D_MODEL), bf),
            pltpu.VMEM((GROUPS, SPATIAL_CHUNK, SPATIAL_CHUNK), bf),
            pltpu.VMEM((TOKEN_TILE + SUBLANES, D_MODEL), jnp.float32),
            pltpu.VMEM((SUB_ROWS, D_MODEL), bf),
            pltpu.VMEM((SUB_ROWS, D_MODEL), jnp.float32),
            pltpu.VMEM((STAGE_SLOTS, STAGE_ROWS, D_MODEL), jnp.float32),
            pltpu.SemaphoreType.DMA((STAGE_SLOTS,)),
        ],
        compiler_params=pltpu.CompilerParams(
            dimension_semantics=("arbitrary",),
            vmem_limit_bytes=VMEM_LIMIT_BYTES),
        name="gmlp_shortconv_block",
    )(x2d, x2d, row(norm_g[0]), row(v_norm_g[0]), w_spatial[0], bias_full,
      conv_w[0], row(final_norm_g), w_in[0], w_branch_a[0], w_branch_b[0], w_out[0])
    return out.reshape(batch, seq, D_MODEL)
```

```python
import functools
import math

import jax
import jax.numpy as jnp
from jax import lax
from jax.experimental import pallas as pl
from jax.experimental.pallas import tpu as pltpu

D_MODEL = 1024
GROUPS = 8
GROUP_DIM = D_MODEL // GROUPS
SPATIAL_CHUNK = 128
CONV_WIDTH = 3
N_SEG = 9
EPS = 1e-6
SUBLANES = 8
SUB_ROWS = 256
TOKEN_TILE = 2 * SUB_ROWS
STAGE_ROWS = 512
STAGE_SLOTS = 3
VMEM_LIMIT_BYTES = 58 * 1024 * 1024

SEG_U, SEG_V, SEG_ZA, SEG_XB, SEG_CB, SEG_BB, SEG_ZB, SEG_GA, SEG_GB = range(N_SEG)
W_PA, W_PB, W_OUT = N_SEG, N_SEG + 1, N_SEG + 2
USE_ORDER = (SEG_V, SEG_U, SEG_ZA, SEG_CB, SEG_XB, SEG_BB, SEG_ZB, W_PA, SEG_GA, SEG_GB, W_PB, W_OUT)


def _rms_scale(x):
    return lax.rsqrt(jnp.mean(x * x, axis=-1, keepdims=True) + EPS)


def _gelu(x):
    return 0.5 * x * (1.0 + lax.erf(x * (1.0 / math.sqrt(2.0))))


def _sigmoid(x):
    return 1.0 / (1.0 + jnp.exp(-x))


def _silu(x):
    return x * _sigmoid(x)


def _dot(a, b):
    return jnp.dot(a, b, preferred_element_type=jnp.float32)


class _WeightStream:
    def __init__(self, hbm_blocks, bf16_blocks, stage, sem):
        self._jobs = [(w, pl.ds(r, STAGE_ROWS)) for w in USE_ORDER
                      for r in range(0, D_MODEL, STAGE_ROWS)]
        self._dst = bf16_blocks
        self._stage = stage
        self._copies = [pltpu.make_async_copy(hbm_blocks[w].at[rows], stage.at[k % STAGE_SLOTS],
                                              sem.at[k % STAGE_SLOTS])
                        for k, (w, rows) in enumerate(self._jobs)]
        self._next = 0
        for copy in self._copies[:STAGE_SLOTS]:
            copy.start()

    def fetch(self, w):
        for _ in range(D_MODEL // STAGE_ROWS):
            k = self._next
            job_w, rows = self._jobs[k]
            assert job_w == w, "weights must be fetched in USE_ORDER"
            self._next += 1
            self._copies[k].wait()
            self._dst[w][rows, :] = self._stage[k % STAGE_SLOTS].astype(jnp.bfloat16)
            if k + STAGE_SLOTS < len(self._jobs):
                self._copies[k + STAGE_SLOTS].start()

    def done(self):
        assert self._next == len(self._jobs)


def _tile_body(x_ref, xn_ref, xn_copy, ng_ref, vg_ref, bias_ref, cw_ref, fg_ref, o_ref, wts, ws_scr,
               hc_scr, ha_scr, va_scr, fetch, first):
    tm = x_ref.shape[0]
    starts = range(0, tm, SUB_ROWS)

    def spatial_mix(vn):
        chunks = []
        for n in range(SUB_ROWS // SPATIAL_CHUNK):
            rows = slice(n * SPATIAL_CHUNK, (n + 1) * SPATIAL_CHUNK)
            cols = [_dot(ws_scr[g], vn[rows, g * GROUP_DIM:(g + 1) * GROUP_DIM])
                    for g in range(GROUPS)]
            chunks.append(jnp.concatenate(cols, axis=1) + bias_ref[...])
        return jnp.concatenate(chunks, axis=0)

    def conv3(r0):
        taps = [hc_scr[SUBLANES - 2 + k + r0:SUBLANES - 2 + k + r0 + SUB_ROWS, :] * cw_ref[k:k + 1, :]
                for k in range(CONV_WIDTH)]
        return taps[0] + taps[1] + taps[2]

    def normed(x):
        return (x * _rms_scale(x) * ng_ref[...]).astype(jnp.bfloat16)

    xs = [x_ref[r0:r0 + SUB_ROWS, :] for r0 in starts]
    hs = [normed(x) if first or k > 0 else ha_scr[...] for k, x in enumerate(xs)]

    fetch(SEG_V)
    v_raws = [_dot(hs[0], wts[SEG_V][...]) if first else va_scr[...]]
    fetch(SEG_U)
    fetch(SEG_ZA)
    ugs = []
    for k, h in enumerate(hs):
        if k > 0:
            v_raws.append(_dot(h, wts[SEG_V][...]))
        ugs.append(_gelu(_dot(h, wts[SEG_U][...])) * _silu(_dot(h, wts[SEG_ZA][...])))
    vs = [_gelu(v) for v in v_raws]
    vns = [(v * _rms_scale(v) * vg_ref[...]).astype(jnp.bfloat16) for v in vs]

    fetch(SEG_CB)
    fetch(SEG_XB)
    mixeds = []
    for r0, h, vn in zip(starts, hs, vns):
        mixeds.append(spatial_mix(vn))
        hc_scr[SUBLANES + r0:SUBLANES + r0 + SUB_ROWS, :] = (
            _dot(h, wts[SEG_CB][...]) * _dot(h, wts[SEG_XB][...]))
    y_as = [(ug * mixed).astype(jnp.bfloat16) for ug, mixed in zip(ugs, mixeds)]
    fetch(SEG_BB)
    fetch(SEG_ZB)
    bgs = [_dot(h, wts[SEG_BB][...]) * _silu(_dot(h, wts[SEG_ZB][...])) for h in hs]
    fetch(W_PA)
    ya_ds = [_dot(y_a, wts[W_PA][...]) for y_a in y_as]
    y_bs = [(bg * conv3(r0)).astype(jnp.bfloat16) for r0, bg in zip(starts, bgs)]
    hc_scr[0:SUBLANES, :] = hc_scr[tm:tm + SUBLANES, :]
    fetch(SEG_GA)
    fetch(SEG_GB)
    sigs = [(_sigmoid(_dot(h, wts[SEG_GA][...])), _sigmoid(_dot(h, wts[SEG_GB][...]))) for h in hs]
    fetch(W_PB)
    yb_ds = [_dot(y_b, wts[W_PB][...]) for y_b in y_bs]

    mergeds = [(sa * ya_d + sb * yb_d).astype(jnp.bfloat16)
               for (sa, sb), ya_d, yb_d in zip(sigs, ya_ds, yb_ds)]
    fetch(W_OUT)
    for r0, x, merged in zip(starts, xs, mergeds):
        y = x + _dot(merged, wts[W_OUT][...])
        o_ref[r0:r0 + SUB_ROWS, :] = y * _rms_scale(y) * fg_ref[...]

    xn_copy.wait()
    h_next = normed(xn_ref[...])
    ha_scr[...] = h_next
    va_scr[...] = _dot(h_next, wts[SEG_V][...])


def _block_kernel(x_ref, ng_ref, vg_ref, ws_ref, bias_ref, cw_ref, fg_ref,
                  x_hbm, win_hbm, pa_hbm, pb_hbm, wo_hbm, o_ref,
                  win_scr, pa_scr, pb_scr, wo_scr, ws_scr, hc_scr, ha_scr, va_scr, xn_scr,
                  stage, stage_sem, xn_sem, *, tiles_per_seq, n_tiles):
    i = pl.program_id(0)
    seg_cols = lambda ref, s: ref.at[:, pl.ds(s * D_MODEL, D_MODEL)]
    wts = [seg_cols(win_scr, s) for s in range(N_SEG)] + [pa_scr, pb_scr, wo_scr]

    next_row = pl.multiple_of(jnp.minimum(i + 1, n_tiles - 1) * TOKEN_TILE, TOKEN_TILE)
    xn_copy = pltpu.make_async_copy(x_hbm.at[pl.ds(next_row, SUB_ROWS)], xn_scr, xn_sem)
    xn_copy.start()
    body = functools.partial(_tile_body, x_ref, xn_scr, xn_copy, ng_ref, vg_ref, bias_ref, cw_ref,
                             fg_ref, o_ref, wts, ws_scr, hc_scr, ha_scr, va_scr)

    @pl.when(i % tiles_per_seq == 0)
    def _():
        hc_scr[0:SUBLANES, :] = jnp.zeros((SUBLANES, D_MODEL), jnp.float32)

    @pl.when(i == 0)
    def _():
        row = lax.broadcasted_iota(jnp.int32, (SPATIAL_CHUNK, SPATIAL_CHUNK), 0)
        col = lax.broadcasted_iota(jnp.int32, (SPATIAL_CHUNK, SPATIAL_CHUNK), 1)
        for g in range(GROUPS):
            ws_scr[g] = jnp.where(row >= col, ws_ref[g], 0.0).astype(jnp.bfloat16)
        hbm = [seg_cols(win_hbm, s) for s in range(N_SEG)] + [pa_hbm, pb_hbm, wo_hbm]
        stream = _WeightStream(hbm, wts, stage, stage_sem)
        body(stream.fetch, first=True)
        stream.done()

    @pl.when(i != 0)
    def _():
        body(lambda w: None, first=False)


def _resident(shape):
    return pl.BlockSpec(shape, lambda i: (0,) * len(shape), pipeline_mode=pl.Buffered(1))


def kernel(x, norm_g, w_in, v_norm_g, w_spatial, b_spatial, conv_w,
           w_branch_a, w_branch_b, w_out, final_norm_g):
    batch, seq, d = x.shape
    depth = norm_g.shape[0]
    assert d == D_MODEL and depth == 1 and seq % TOKEN_TILE == 0
    assert w_in.shape == (depth, D_MODEL, N_SEG * D_MODEL)
    tokens = batch * seq
    bf = jnp.bfloat16

    bias_full = jnp.repeat(b_spatial[0].T, GROUP_DIM, axis=1)
    row = lambda a: a.reshape(1, D_MODEL)
    in_hbm = pl.BlockSpec(memory_space=pl.ANY)
    n_tiles = tokens // TOKEN_TILE
    x2d = x.reshape(tokens, D_MODEL)

    out = pl.pallas_call(
        functools.partial(_block_kernel, tiles_per_seq=seq // TOKEN_TILE, n_tiles=n_tiles),
        grid=(n_tiles,),
        in_specs=[
            pl.BlockSpec((TOKEN_TILE, D_MODEL), lambda i: (i, 0)),
            _resident((1, D_MODEL)),
            _resident((1, D_MODEL)),
            _resident((GROUPS, SPATIAL_CHUNK, SPATIAL_CHUNK)),
            _resident((SPATIAL_CHUNK, D_MODEL)),
            _resident((CONV_WIDTH, D_MODEL)),
            _resident((1, D_MODEL)),
            in_hbm, in_hbm, in_hbm, in_hbm, in_hbm,
        ],
        out_specs=pl.BlockSpec((TOKEN_TILE, D_MODEL), lambda i: (i, 0)),
        out_shape=jax.ShapeDtypeStruct((tokens, D_MODEL), jnp.float32),
        scratch_shapes=[
            pltpu.VMEM((D_MODEL, N_SEG * D_MODEL), bf),
            pltpu.VMEM((D_MODEL, D_MODEL), bf),
            pltpu.VMEM((D_MODEL, D_MODEL), bf),
            pltpu.VMEM((D_MODEL, D_MODEL), bf),
            pltpu.VMEM((GROUPS, SPATIAL_CHUNK, SPATIAL_CHUNK), bf),
            pltpu.VMEM((TOKEN_TILE + SUBLANES, D_MODEL), jnp.float32),
            pltpu.VMEM((SUB_ROWS, D_MODEL), bf),
            pltpu.VMEM((SUB_ROWS, D_MODEL), jnp.float32),
            pltpu.VMEM((SUB_ROWS, D_MODEL), jnp.float32),
            pltpu.VMEM((STAGE_SLOTS, STAGE_ROWS, D_MODEL), jnp.float32),
            pltpu.SemaphoreType.DMA((STAGE_SLOTS,)),
            pltpu.SemaphoreType.DMA(()),
        ],
        compiler_params=pltpu.CompilerParams(
            dimension_semantics=("arbitrary",),
            vmem_limit_bytes=VMEM_LIMIT_BYTES),
        name="gmlp_shortconv_block",
    )(x2d, row(norm_g[0]), row(v_norm_g[0]), w_spatial[0], bias_full,
      conv_w[0], row(final_norm_g), x2d, w_in[0], w_branch_a[0], w_branch_b[0], w_out[0])
    return out.reshape(batch, seq, D_MODEL)
```
